```python
import math
import jax, jax.numpy as jnp
from jax import lax
import numpy as np

D_MODEL = 2048
BATCH = 1
SEQ = 16384
DEPTH = 4

CHUNK = 64
N_META = 16
SB_BLOCK = 128
PAD_FRONT = SB_BLOCK - N_META
PREFIX = PAD_FRONT + N_META

GLA_HEADS = 4
GLA_DK = D_MODEL // 2 // GLA_HEADS
GLA_DV = D_MODEL // GLA_HEADS
GLA_GATE_RANK = 16
GLA_TAU = 16.0
GLA_QK = GLA_HEADS * GLA_DK
GLA_V = GLA_HEADS * GLA_DV

SB_DH = 128
SB_HEADS = D_MODEL // 2 // SB_DH
SB_W = SB_HEADS * SB_DH

SSM_EXPAND = 2
D_INNER = SSM_EXPAND * D_MODEL
SSM_HEADDIM = 64
SSM_HEADS = D_INNER // SSM_HEADDIM
SSM_GROUPS = 8
SSM_HPG = SSM_HEADS // SSM_GROUPS
D_STATE = 128
CONV_K = 4
CONV_DIM = D_INNER + 2 * SSM_GROUPS * D_STATE

EVEN_SPLITS = (GLA_QK, GLA_QK, GLA_V, GLA_V, GLA_GATE_RANK, SB_W, SB_W, SB_W, SB_W)
EVEN_IN = sum(EVEN_SPLITS)
EVEN_MIX = GLA_V + SB_W
ODD_SPLITS = (D_INNER, CONV_DIM, SSM_HEADS)
ODD_IN = sum(ODD_SPLITS)

DEEPNORM_ALPHA = (2 * DEPTH) ** 0.25
DEEPNORM_BETA = (8 * DEPTH) ** -0.25
LN_EPS = 1e-5
RMS_EPS = 1e-6

kernel_name = 'hybrid_gla_stickbreak_ssd_deepnorm'


def _split(a, sizes):
    return jnp.split(a, np.cumsum(sizes)[:-1].tolist(), axis=-1)


def _layer_norm(x, g, b):
    mu = jnp.mean(x, -1, keepdims=True)
    xc = x - mu
    var = jnp.mean(xc * xc, -1, keepdims=True)
    return xc * lax.rsqrt(var + LN_EPS) * g + b


def _rms_norm_groups(y, w, n_groups):
    shp = y.shape
    yg = y.reshape(*shp[:-1], n_groups, shp[-1] // n_groups)
    yg = yg * lax.rsqrt(jnp.mean(yg * yg, -1, keepdims=True) + RMS_EPS)
    return yg.reshape(shp) * w


def _to_chunks(a):
    return a.reshape(a.shape[0], a.shape[1] // CHUNK, CHUNK, *a.shape[2:])


def _causal_depthwise_conv(u, w, b):
    out = lax.conv_general_dilated(
        u, w.astype(u.dtype)[:, None, :], window_strides=(1,),
        padding=((CONV_K - 1, 0),), dimension_numbers=('NWC', 'WIO', 'NWC'),
        feature_group_count=u.shape[-1])
    return out + b


def _gla(q, k, v, log_a):
    bsz, t_len, n_h, d_k = q.shape
    d_v = v.shape[-1]
    q, k, v, log_a = (_to_chunks(t) for t in (q, k, v, log_a))
    g_cum = jnp.cumsum(log_a, axis=2)
    g_end = g_cum[:, :, -1]
    q_dec = q * jnp.exp(g_cum)
    k_inv = k * jnp.exp(-g_cum)
    k_end = k * jnp.exp(g_end[:, :, None] - g_cum)
    causal = jnp.tril(jnp.ones((CHUNK, CHUNK), dtype=bool))
    scores = jnp.einsum('bcthk,bcshk->bchts', q_dec, k_inv)
    scores = jnp.where(causal, scores, 0.0)
    o_intra = jnp.einsum('bchts,bcshv->bcthv', scores, v)

    def step(state, inp):
        qd, ke, vv, dec = inp
        o = jnp.einsum('bthk,bhkv->bthv', qd, state)
        state = dec[..., None] * state + jnp.einsum('bshk,bshv->bhkv', ke, vv)
        return state, o

    xs = tuple(jnp.moveaxis(t, 1, 0) for t in (q_dec, k_end, v, jnp.exp(g_end)))
    s0 = jnp.zeros((bsz, n_h, d_k, d_v), jnp.float32)
    _, o_inter = lax.scan(step, s0, xs)
    o = o_intra + jnp.moveaxis(o_inter, 0, 1)
    return o.reshape(bsz, t_len, n_h, d_v)


def _stick_breaking(q, k, v, valid):
    bsz, t_len, n_h, d_h = q.shape
    q = jnp.transpose(q, (0, 2, 1, 3)) * (d_h ** -0.5)
    k = jnp.transpose(k, (0, 2, 1, 3))
    v = jnp.transpose(v, (0, 2, 1, 3))
    blk = jnp.arange(SB_BLOCK)
    later_in_block = (blk[:, None] > blk[None, :]).astype(jnp.float32)
    outs = []
    for i in range(t_len // SB_BLOCK):
        n_k = i + 1
        k_len = n_k * SB_BLOCK
        qb = q[:, :, i * SB_BLOCK:(i + 1) * SB_BLOCK]
        z = jnp.einsum('bhqd,bhkd->bhqk', qb, k[:, :, :k_len]).astype(jnp.float32)
        q_pos = i * SB_BLOCK + blk
        mask = (jnp.arange(k_len)[None, :] < q_pos[:, None]) & valid[None, :k_len]
        log_stay = jnp.where(mask, jax.nn.log_sigmoid(-z), 0.0)
        ls = log_stay.reshape(bsz, n_h, SB_BLOCK, n_k, SB_BLOCK)
        within = jnp.einsum('bhqnk,kl->bhqnl', ls, later_in_block)
        tot = jnp.sum(ls, axis=-1)
        later = lax.cumsum(tot, axis=3, reverse=True) - tot
        log_between = (within + later[..., None]).reshape(bsz, n_h, SB_BLOCK, k_len)
        w = jnp.where(mask, jnp.exp(jax.nn.log_sigmoid(z) + log_between), 0.0)
        outs.append(jnp.einsum('bhqk,bhkd->bhqd', w, v[:, :, :k_len].astype(jnp.float32)))
    out = jnp.concatenate(outs, axis=2)
    return jnp.transpose(out, (0, 2, 1, 3))


def _ssd(x, b_in, c_in, dt, a_log, d_skip):
    bsz, t_len = x.shape[:2]
    a_neg = -jnp.exp(a_log).reshape(SSM_GROUPS, SSM_HPG)
    xdt = x * dt[..., None]
    x_c, xdt, b_c, c_c, a_c = (_to_chunks(t) for t in (x, xdt, b_in, c_in, dt * a_neg))
    a_cum = jnp.cumsum(a_c, axis=2)
    a_end = a_cum[:, :, -1]
    causal = jnp.tril(jnp.ones((CHUNK, CHUNK), dtype=bool))
    seg = a_cum[:, :, :, None] - a_cum[:, :, None]
    l_mat = jnp.exp(jnp.where(causal[:, :, None, None], seg, -jnp.inf))
    cb = jnp.einsum('bctgn,bcsgn->bctsg', c_c, b_c)
    y_diag = jnp.einsum('bctsgh,bcsghp->bctghp', l_mat * cb[..., None], xdt)
    decay_s = jnp.exp(a_end[:, :, None] - a_cum)

    def step(hs, inp):
        c_t, in_dec, b_s, dec_s, xd, dec_chunk = inp
        y = jnp.einsum('btgn,bghpn->btghp', c_t, hs) * in_dec[..., None]
        hs = dec_chunk[..., None, None] * hs + jnp.einsum('bsgn,bsgh,bsghp->bghpn', b_s, dec_s, xd)
        return hs, y

    xs = tuple(jnp.moveaxis(t, 1, 0) for t in (c_c, jnp.exp(a_cum), b_c, decay_s, xdt, jnp.exp(a_end)))
    h0 = jnp.zeros((bsz, SSM_GROUPS, SSM_HPG, SSM_HEADDIM, D_STATE), jnp.float32)
    _, y_off = lax.scan(step, h0, xs)
    y = y_diag + jnp.moveaxis(y_off, 0, 1) + d_skip.reshape(SSM_GROUPS, SSM_HPG)[..., None] * x_c
    return y.reshape(bsz, t_len, D_INNER)


def _even_mixer(h, valid, w_in, w_gate2, b_gate, norm_w, w_out):
    bsz, t_len, _ = h.shape
    vmask = valid[None, :, None]
    proj = jnp.einsum('btd,de->bte', h, w_in).astype(jnp.float32)
    gq, gk, gv, gr, gu, sq, sk, sv, sr = _split(proj, EVEN_SPLITS)
    gq = gq.reshape(bsz, t_len, GLA_HEADS, GLA_DK) * (GLA_DK ** -0.5)
    gk = jnp.where(vmask, gk, 0.0).reshape(bsz, t_len, GLA_HEADS, GLA_DK)
    gv = gv.reshape(bsz, t_len, GLA_HEADS, GLA_DV)
    log_a = jax.nn.log_sigmoid(gu @ w_gate2 + b_gate) / GLA_TAU
    log_a = jnp.where(vmask, log_a, 0.0).reshape(bsz, t_len, GLA_HEADS, GLA_DK)
    o_gla = _gla(gq, gk, gv, log_a).reshape(bsz, t_len, GLA_V)
    o_gla = _rms_norm_groups(o_gla, norm_w, GLA_HEADS) * jax.nn.silu(gr)
    shp = (bsz, t_len, SB_HEADS, SB_DH)
    o_sb = _stick_breaking(sq.reshape(shp), sk.reshape(shp), sv.reshape(shp), valid)
    o_sb = o_sb.reshape(bsz, t_len, SB_W) * jax.nn.silu(sr)
    mix = jnp.concatenate([o_gla, o_sb], axis=-1)
    return jnp.einsum('bte,ed->btd', mix, w_out)


def _odd_mixer(h, valid, w_in, conv_w, conv_b, dt_bias, a_log, d_skip, norm_w, w_out):
    bsz, t_len, _ = h.shape
    vmask = valid[None, :, None]
    proj = jnp.einsum('btd,de->bte', h, w_in).astype(jnp.float32)
    z, xbc, dt_raw = _split(proj, ODD_SPLITS)
    xbc = jnp.where(vmask, xbc, 0.0)
    xbc = jax.nn.silu(_causal_depthwise_conv(xbc, conv_w, conv_b))
    xs, bs, cs = _split(xbc, (D_INNER, SSM_GROUPS * D_STATE, SSM_GROUPS * D_STATE))
    xs = jnp.where(vmask, xs, 0.0).reshape(bsz, t_len, SSM_GROUPS, SSM_HPG, SSM_HEADDIM)
    bs = bs.reshape(bsz, t_len, SSM_GROUPS, D_STATE)
    cs = cs.reshape(bsz, t_len, SSM_GROUPS, D_STATE)
    dt = jax.nn.softplus(dt_raw + dt_bias).reshape(bsz, t_len, SSM_GROUPS, SSM_HPG)
    y = _ssd(xs, bs, cs, dt, a_log, d_skip)
    y = _rms_norm_groups(y * jax.nn.silu(z), norm_w, SSM_GROUPS)
    return jnp.einsum('bte,ed->btd', y, w_out)


def setup_inputs(seed: int = 0) -> dict:
    key = jax.random.key(seed)
    ks = jax.random.split(key, 17)
    f32 = jnp.float32
    n_even = (DEPTH + 1) // 2
    n_odd = DEPTH // 2

    def nrm(k, shape):
        return jax.random.normal(k, shape, f32)

    dt0 = jnp.exp(jax.random.uniform(ks[10], (n_odd, SSM_HEADS), f32, math.log(1e-3), math.log(1e-1)))
    return {
        'x': nrm(ks[0], (BATCH, SEQ, D_MODEL)),
        'meta': nrm(ks[1], (N_META, D_MODEL)),
        'ev_w_in': nrm(ks[2], (n_even, D_MODEL, EVEN_IN)) * D_MODEL ** -0.5,
        'ev_gla_w_gate2': nrm(ks[3], (n_even, GLA_GATE_RANK, GLA_QK)) * GLA_GATE_RANK ** -0.5,
        'ev_gla_b_gate': 0.1 * nrm(ks[4], (n_even, GLA_QK)),
        'ev_gla_norm_w': 1.0 + 0.02 * nrm(ks[5], (n_even, GLA_V)),
        'ev_w_out': nrm(ks[6], (n_even, EVEN_MIX, D_MODEL)) * (EVEN_MIX ** -0.5 * DEEPNORM_BETA),
        'od_w_in': nrm(ks[7], (n_odd, D_MODEL, ODD_IN)) * D_MODEL ** -0.5,
        'od_conv_w': nrm(ks[8], (n_odd, CONV_K, CONV_DIM)) * CONV_K ** -0.5,
        'od_conv_b': 0.02 * nrm(ks[9], (n_odd, CONV_DIM)),
        'od_dt_bias': dt0 + jnp.log(-jnp.expm1(-dt0)),
        'od_a_log': jnp.log(jax.random.uniform(ks[11], (n_odd, SSM_HEADS), f32, 1.0, 16.0)),
        'od_d_skip': 1.0 + 0.02 * nrm(ks[12], (n_odd, SSM_HEADS)),
        'od_norm_w': 1.0 + 0.02 * nrm(ks[13], (n_odd, D_INNER)),
        'od_w_out': nrm(ks[14], (n_odd, D_INNER, D_MODEL)) * (D_INNER ** -0.5 * DEEPNORM_BETA),
        'ln_g': 1.0 + 0.02 * nrm(ks[15], (DEPTH, D_MODEL)),
        'ln_b': 0.02 * nrm(ks[16], (DEPTH, D_MODEL)),
    }


def reference(x, meta, ev_w_in, ev_gla_w_gate2, ev_gla_b_gate, ev_gla_norm_w, ev_w_out,
              od_w_in, od_conv_w, od_conv_b, od_dt_bias, od_a_log, od_d_skip, od_norm_w,
              od_w_out, ln_g, ln_b):
    bsz = x.shape[0]
    dtype = x.dtype
    pad = jnp.zeros((bsz, PAD_FRONT, D_MODEL), dtype)
    m = jnp.broadcast_to(meta.astype(dtype)[None], (bsz, N_META, D_MODEL))
    h = jnp.concatenate([pad, m, x], axis=1)
    valid = jnp.arange(h.shape[1]) >= PAD_FRONT
    for layer in range(DEPTH):
        j = layer // 2
        if layer % 2 == 0:
            f = _even_mixer(h, valid, ev_w_in[j], ev_gla_w_gate2[j], ev_gla_b_gate[j],
                            ev_gla_norm_w[j], ev_w_out[j])
        else:
            f = _odd_mixer(h, valid, od_w_in[j], od_conv_w[j], od_conv_b[j], od_dt_bias[j],
                           od_a_log[j], od_d_skip[j], od_norm_w[j], od_w_out[j])
        h = _layer_norm(DEEPNORM_ALPHA * h.astype(jnp.float32) + f, ln_g[layer], ln_b[layer]).astype(dtype)
    return h[:, PREFIX:]
```

```python
import functools

import jax
import jax.numpy as jnp
import numpy as np
from jax import lax
from jax.experimental import pallas as pl
from jax.experimental.pallas import tpu as pltpu

F32 = jnp.float32
BF16 = jnp.bfloat16

D_MODEL = 2048
SEQ = 16384
DEPTH = 4
CHUNK = 64
N_META = 16
PAD_FRONT = 112

GLA_HEADS = 4
GLA_DK = 256
GLA_DV = 512
GLA_GATE_RANK = 16
GLA_TAU = 16.0
GLA_QK = GLA_HEADS * GLA_DK
GLA_V = GLA_HEADS * GLA_DV

SB_DH = 128
SB_HEADS = 8
SB_W = SB_HEADS * SB_DH

D_INNER = 4096
SSM_HEADDIM = 64
SSM_HEADS = 64
SSM_GROUPS = 8
SSM_HPG = 8
D_STATE = 128
CONV_K = 4
GROUP_W = SSM_HPG * SSM_HEADDIM

DEEPNORM_ALPHA = (2 * DEPTH) ** 0.25
LN_EPS = 1e-5
RMS_EPS = 1e-6

LANES = 128
ROW_BLOCK = 256
EXTRA_PAD = 384
VMEM_LIMIT = 56 * 1024 * 1024


def _cparams(sem):
    return pltpu.CompilerParams(dimension_semantics=sem, vmem_limit_bytes=VMEM_LIMIT)


def _dot(a, b):
    return jnp.dot(a, b, preferred_element_type=F32)


def _dot_nt(a, b):
    return lax.dot_general(a, b, (((1,), (1,)), ((), ())), preferred_element_type=F32)


def _dot_tn(a, b):
    return lax.dot_general(a, b, (((0,), (0,)), ((), ())), preferred_element_type=F32)


def _split3(x):
    hi = x.astype(BF16)
    r1 = x - hi.astype(F32)
    mid = r1.astype(BF16)
    lo = (r1 - mid.astype(F32)).astype(BF16)
    return hi, mid, lo


def _log_sigmoid(x):
    return jnp.minimum(x, 0.0) - jnp.log1p(jnp.exp(-jnp.abs(x)))


def _softplus(x):
    return jnp.maximum(x, 0.0) + jnp.log1p(jnp.exp(-jnp.abs(x)))


def _silu(x):
    return x / (1.0 + jnp.exp(-x))


def _iota(shape, dim):
    return lax.broadcasted_iota(jnp.int32, shape, dim)


def _mm_body(x_ref, w_ref, o_ref):
    o_ref[...] = _dot(x_ref[...], w_ref[...]).astype(o_ref.dtype)


def _matmul(x, w, out_dtype, tm, tn):
    m, k = x.shape
    n = w.shape[1]
    return pl.pallas_call(
        _mm_body,
        out_shape=jax.ShapeDtypeStruct((m, n), out_dtype),
        grid=(m // tm, n // tn),
        in_specs=[pl.BlockSpec((tm, k), lambda i, j: (i, 0)),
                  pl.BlockSpec((k, tn), lambda i, j: (0, j))],
        out_specs=pl.BlockSpec((tm, tn), lambda i, j: (i, j)),
        compiler_params=_cparams(("parallel", "parallel")),
        name="in_proj",
    )(x, w)


def _out_ln_body(*refs, n_lhs):
    lhs = refs[:n_lhs]
    ws = refs[n_lhs:2 * n_lhs]
    h_ref, g_ref, b_ref, o_ref, ob_ref = refs[2 * n_lhs:]
    f = _dot(lhs[0][...], ws[0][...])
    for a, w in zip(lhs[1:], ws[1:]):
        f = f + _dot(a[...], w[...])
    y = DEEPNORM_ALPHA * h_ref[...] + f
    mu = jnp.mean(y, axis=-1, keepdims=True)
    yc = y - mu
    var = jnp.mean(yc * yc, axis=-1, keepdims=True)
    out = yc * lax.rsqrt(var + LN_EPS) * g_ref[...] + b_ref[...]
    o_ref[...] = out
    ob_ref[...] = out.astype(BF16)


def _out_ln(lhs_list, w_list, h, g, b, tm):
    t, d = h.shape
    n_lhs = len(lhs_list)
    in_specs = [pl.BlockSpec((tm, a.shape[1]), lambda i: (i, 0)) for a in lhs_list]
    in_specs += [pl.BlockSpec(w.shape, lambda i: (0, 0)) for w in w_list]
    in_specs += [pl.BlockSpec((tm, d), lambda i: (i, 0)),
                 pl.BlockSpec((1, d), lambda i: (0, 0)),
                 pl.BlockSpec((1, d), lambda i: (0, 0))]
    return pl.pallas_call(
        functools.partial(_out_ln_body, n_lhs=n_lhs),
        out_shape=(jax.ShapeDtypeStruct((t, d), F32), jax.ShapeDtypeStruct((t, d), BF16)),
        grid=(t // tm,),
        in_specs=in_specs,
        out_specs=(pl.BlockSpec((tm, d), lambda i: (i, 0)), pl.BlockSpec((tm, d), lambda i: (i, 0))),
        compiler_params=_cparams(("parallel",)),
        name="out_proj_ln",
    )(*lhs_list, *w_list, h, g.reshape(1, d), b.reshape(1, d))


def _gla_body(q_ref, k_ref, v_ref, gu_ref, gr_ref, wg_ref, bg_ref, nw_ref, tri_ref,
              o_ref, st_ref, *, rows, valid_start):
    i = pl.program_id(1)

    @pl.when(i == 0)
    def _():
        st_ref[...] = jnp.zeros_like(st_ref)

    ridx = i * rows + _iota((rows, 1), 0)
    valid = ridx >= valid_start
    gate = _dot(gu_ref[...].astype(BF16), wg_ref[...]) + bg_ref[...]
    log_a = jnp.where(valid, _log_sigmoid(gate) / GLA_TAU, 0.0)
    k = jnp.where(valid, k_ref[...], 0.0)
    q = q_ref[...] * (GLA_DK ** -0.5)

    hi, mid, lo = _split3(log_a)
    p = _dot(tri_ref[...], jnp.concatenate([hi, mid, lo], axis=1))
    g_cum_all = p[:, :GLA_DK] + p[:, GLA_DK:2 * GLA_DK] + p[:, 2 * GLA_DK:]

    tmask = _iota((CHUNK, CHUNK), 0) >= _iota((CHUNK, CHUNK), 1)
    nw = nw_ref[...]
    for c in range(rows // CHUNK):
        sl = slice(c * CHUNK, (c + 1) * CHUNK)
        g_cum = g_cum_all[sl]
        g_end = g_cum[CHUNK - 1:CHUNK]
        q_dec = (q[sl] * jnp.exp(g_cum)).astype(BF16)
        k_inv = (k[sl] * jnp.exp(-g_cum)).astype(BF16)
        k_end = (k[sl] * jnp.exp(g_end - g_cum)).astype(BF16)
        vv = v_ref[sl, :]
        s = jnp.where(tmask, _dot_nt(q_dec, k_inv), 0.0)
        st = st_ref[...]
        o = _dot(s.astype(BF16), vv) + _dot_nt(q_dec, st.astype(BF16))
        st_ref[...] = st * jnp.exp(g_end) + _dot_tn(vv, k_end)
        ms = jnp.mean(o * o, axis=-1, keepdims=True)
        y = o * lax.rsqrt(ms + RMS_EPS) * nw * _silu(gr_ref[sl, :])
        o_ref[sl, :] = y.astype(BF16)


def _gla(pf, pb, gu, wg, bg, nw, valid_start):
    t = pf.shape[0]
    rows = ROW_BLOCK
    r = np.arange(rows)
    tri = ((r[:, None] >= r[None, :]) & (r[:, None] // CHUNK == r[None, :] // CHUNK))
    tri = jnp.asarray(tri, BF16)
    nkb = GLA_QK // GLA_DK
    return pl.pallas_call(
        functools.partial(_gla_body, rows=rows, valid_start=valid_start),
        out_shape=jax.ShapeDtypeStruct((t, GLA_V), BF16),
        grid=(GLA_HEADS, t // rows),
        in_specs=[
            pl.BlockSpec((rows, GLA_DK), lambda h, i: (i, h)),
            pl.BlockSpec((rows, GLA_DK), lambda h, i: (i, nkb + h)),
            pl.BlockSpec((rows, GLA_DV), lambda h, i: (i, h)),
            pl.BlockSpec((rows, LANES), lambda h, i: (i, 0)),
            pl.BlockSpec((rows, GLA_DV), lambda h, i: (i, 2 * GLA_QK // GLA_DV + h)),
            pl.BlockSpec((LANES, GLA_DK), lambda h, i: (0, h)),
            pl.BlockSpec((1, GLA_DK), lambda h, i: (0, h)),
            pl.BlockSpec((1, GLA_DV), lambda h, i: (0, h)),
            pl.BlockSpec((rows, rows), lambda h, i: (0, 0)),
        ],
        out_specs=pl.BlockSpec((rows, GLA_DV), lambda h, i: (i, h)),
        scratch_shapes=[pltpu.VMEM((GLA_DV, GLA_DK), F32)],
        compiler_params=_cparams(("parallel", "arbitrary")),
        name="gla",
    )(pf, pf, pb, gu, pf, wg, bg, nw, tri)


def _sb_body(q_ref, k_ref, v_ref, sr_ref, lt_ref, o_ref, acc_ref, car_ref, *, blk, valid_start):
    i = pl.program_id(1)
    q = q_ref[...]
    acc_ref[...] = jnp.zeros_like(acc_ref)
    car_ref[...] = jnp.zeros_like(car_ref)
    scale = SB_DH ** -0.5

    def step(j, masked):
        start = pl.multiple_of(j * blk, blk)
        kb = k_ref[pl.ds(start, blk), :]
        vb = v_ref[pl.ds(start, blk), :]
        z = _dot_nt(q, kb) * scale
        sp = _softplus(z)
        if masked:
            qi = i * blk + _iota((blk, blk), 0)
            ki = j * blk + _iota((blk, blk), 1)
            mask = (ki < qi) & (ki >= valid_start)
            ls = jnp.where(mask, -sp, 0.0)
        else:
            ls = -sp
        within = _dot(ls.astype(BF16), lt_ref[...])
        logw = (z - sp) + (within + car_ref[...])
        w = jnp.exp(logw)
        if masked:
            w = jnp.where(mask, w, 0.0)
        acc_ref[...] += _dot(w.astype(BF16), vb)
        car_ref[...] += within[:, 0:1] + ls[:, 0:1]

    first_kv = valid_start // blk

    @pl.when(i >= first_kv)
    def _():
        step(i, True)

    def body(n, c):
        step(i - 1 - n, False)
        return c

    lax.fori_loop(0, jnp.maximum(i - 1 - first_kv, 0), body, 0)

    @pl.when(i > first_kv)
    def _():
        step(first_kv, True)

    o_ref[...] = (acc_ref[...] * _silu(sr_ref[...])).astype(BF16)


def _sb(pf, pb, valid_start):
    t = pf.shape[0]
    blk = ROW_BLOCK
    r = np.arange(blk)
    later = jnp.asarray(r[:, None] > r[None, :], BF16)
    q0 = GLA_V // SB_DH
    k0 = q0 + SB_HEADS
    v0 = k0 + SB_HEADS
    r0 = (2 * GLA_QK + GLA_V) // SB_DH
    return pl.pallas_call(
        functools.partial(_sb_body, blk=blk, valid_start=valid_start),
        out_shape=jax.ShapeDtypeStruct((t, SB_W), BF16),
        grid=(SB_HEADS, t // blk),
        in_specs=[
            pl.BlockSpec((blk, SB_DH), lambda h, i: (i, q0 + h)),
            pl.BlockSpec((t, SB_DH), lambda h, i: (0, k0 + h)),
            pl.BlockSpec((t, SB_DH), lambda h, i: (0, v0 + h)),
            pl.BlockSpec((blk, SB_DH), lambda h, i: (i, r0 + h)),
            pl.BlockSpec((blk, blk), lambda h, i: (0, 0)),
        ],
        out_specs=pl.BlockSpec((blk, SB_DH), lambda h, i: (i, h)),
        scratch_shapes=[pltpu.VMEM((blk, SB_DH), F32), pltpu.VMEM((blk, 1), F32)],
        compiler_params=_cparams(("parallel", "arbitrary")),
        name="stick_breaking",
    )(pb, pb, pb, pf, later)


def _conv_silu(u, halo, w_ref, b_ref, buf_ref, rows, width):
    buf_ref[0:8, 0:width] = halo
    buf_ref[8:8 + rows, 0:width] = u
    acc = b_ref[...] + w_ref[3:4, :] * u
    for j in range(CONV_K - 1):
        acc = acc + w_ref[j:j + 1, :] * buf_ref[5 + j:5 + j + rows, 0:width]
    return _silu(acc)


def _ssd_body(z_ref, x_ref, b_ref, c_ref, xh_ref, bh_ref, ch_ref, dt_ref,
              wx_ref, wb_ref, wc_ref, bx_ref, bb_ref, bc_ref,
              dtb_ref, aneg_ref, dsk_ref, nw_ref, tri_ref, e_ref,
              o_ref, hs_ref, dts_ref, acs_ref, buf_ref, *, rows, valid_start):
    i = pl.program_id(0)
    g = pl.program_id(1)

    @pl.when(i == 0)
    def _():
        hs_ref[g] = jnp.zeros((D_STATE, GROUP_W), F32)

    @pl.when(g == 0)
    def _():
        dt = _softplus(dt_ref[...] + dtb_ref[...])
        a = dt * aneg_ref[...]
        hi, mid, lo = _split3(a)
        p = _dot(tri_ref[...], jnp.concatenate([hi, mid, lo], axis=1))
        dts_ref[...] = dt
        acs_ref[...] = p[:, :LANES] + p[:, LANES:2 * LANES] + p[:, 2 * LANES:]

    ridx = i * rows + _iota((rows, 1), 0)
    valid = ridx >= valid_start
    hvalid = (i * rows - 8 + _iota((8, 1), 0)) >= valid_start

    def prep(u_ref, h_ref, w_ref, bias_ref, width):
        u = jnp.where(valid, u_ref[...], 0.0)
        hal = jnp.where(hvalid, h_ref[...], 0.0)
        return _conv_silu(u, hal, w_ref, bias_ref, buf_ref, rows, width)

    xs = jnp.where(valid, prep(x_ref, xh_ref, wx_ref, bx_ref, GROUP_W), 0.0)
    bs = prep(b_ref, bh_ref, wb_ref, bb_ref, D_STATE)
    cs = prep(c_ref, ch_ref, wc_ref, bc_ref, D_STATE)

    lane_s = _iota((CHUNK, GROUP_W), 1) & (CHUNK - 1)
    row_t = _iota((CHUNK, GROUP_W), 0)
    diag_rep = lane_s == row_t
    causal_rep = lane_s <= row_t
    pair_mask = (_iota((LANES, LANES), 0) // CHUNK) == (_iota((LANES, LANES), 1) // CHUNK)
    e = e_ref[0]
    dsk = dsk_ref[...]
    nw = nw_ref[...]

    for c in range(rows // CHUNK):
        sl = slice(c * CHUNK, (c + 1) * CHUNK)
        both = jnp.concatenate([dts_ref[sl, :], acs_ref[sl, :]], axis=0)
        hi, mid, lo = _split3(both)
        pe = _dot(jnp.concatenate([hi, mid, lo], axis=0), e)
        ex = pe[0:2 * CHUNK] + pe[2 * CHUNK:4 * CHUNK] + pe[4 * CHUNK:6 * CHUNK]
        dt_col = ex[0:CHUNK]
        ac_col = ex[CHUNK:2 * CHUNK]
        ac_row = jnp.sum(jnp.where(diag_rep, ac_col, 0.0), axis=0, keepdims=True)
        dt_row = jnp.sum(jnp.where(diag_rep, dt_col, 0.0), axis=0, keepdims=True)
        ac_end = ac_col[CHUNK - 1:CHUNK]

        x_c = xs[sl]
        b_c = bs[sl].astype(BF16)
        c_c = cs[sl].astype(BF16)
        cb = _dot_nt(c_c, jnp.concatenate([b_c] * SSM_HPG, axis=0))
        lmat = jnp.exp(jnp.where(causal_rep, ac_col - ac_row, -jnp.inf))
        m = (lmat * cb * dt_row).astype(BF16)
        x_b = x_c.astype(BF16)
        parts = []
        for pr in range(GROUP_W // LANES):
            ps = slice(pr * LANES, (pr + 1) * LANES)
            xp = x_b[:, ps]
            xbd = jnp.where(pair_mask, jnp.concatenate([xp, xp], axis=0), 0.0).astype(BF16)
            parts.append(_dot(m[:, ps], xbd))
        y = jnp.concatenate(parts, axis=1)

        hs = hs_ref[g]
        y = y + _dot(c_c, hs.astype(BF16)) * jnp.exp(ac_col)
        xw = (x_c * dt_col * jnp.exp(ac_end - ac_col)).astype(BF16)
        hs_ref[g] = hs * jnp.exp(ac_end) + _dot_tn(b_c, xw)
        y = y + dsk * x_c

        y = y * _silu(z_ref[sl, :])
        ms = jnp.mean(y * y, axis=-1, keepdims=True)
        o_ref[sl, :] = (y * lax.rsqrt(ms + RMS_EPS) * nw).astype(BF16)


def _ssd(pf, dt_raw, conv_w, conv_b, dt_bias, a_log, d_skip, norm_w, valid_start):
    t = pf.shape[0]
    rows = ROW_BLOCK
    r = np.arange(rows)
    tri = jnp.asarray((r[:, None] >= r[None, :]) & (r[:, None] // CHUNK == r[None, :] // CHUNK), BF16)
    hh = np.arange(LANES)
    lane_head = np.arange(GROUP_W) // SSM_HEADDIM
    e = np.stack([(hh[:, None] == (gi * SSM_HPG + lane_head)[None, :]) for gi in range(SSM_GROUPS)])
    e = jnp.asarray(e, BF16)

    def pad_heads(v):
        return jnp.pad(v.astype(F32), (0, LANES - SSM_HEADS)).reshape(1, LANES)

    dtb = pad_heads(dt_bias)
    aneg = pad_heads(-jnp.exp(a_log))
    dsk = jnp.repeat(d_skip.astype(F32), SSM_HEADDIM).reshape(1, D_INNER)
    nw = norm_w.reshape(1, D_INNER)
    cb = conv_b.reshape(1, -1)
    zb = 0
    xb = D_INNER // GROUP_W
    bb = 2 * D_INNER // D_STATE
    cb0 = bb + SSM_GROUPS
    hb = rows // 8
    wxb = 0
    wbb = D_INNER // D_STATE
    wcb = wbb + SSM_GROUPS

    def halo(i):
        return jnp.maximum(i * hb - 1, 0)

    return pl.pallas_call(
        functools.partial(_ssd_body, rows=rows, valid_start=valid_start),
        out_shape=jax.ShapeDtypeStruct((t, D_INNER), BF16),
        grid=(t // rows, SSM_GROUPS),
        in_specs=[
            pl.BlockSpec((rows, GROUP_W), lambda i, g: (i, zb + g)),
            pl.BlockSpec((rows, GROUP_W), lambda i, g: (i, xb + g)),
            pl.BlockSpec((rows, D_STATE), lambda i, g: (i, bb + g)),
            pl.BlockSpec((rows, D_STATE), lambda i, g: (i, cb0 + g)),
            pl.BlockSpec((8, GROUP_W), lambda i, g: (halo(i), xb + g)),
            pl.BlockSpec((8, D_STATE), lambda i, g: (halo(i), bb + g)),
            pl.BlockSpec((8, D_STATE), lambda i, g: (halo(i), cb0 + g)),
            pl.BlockSpec((rows, LANES), lambda i, g: (i, 0)),
            pl.BlockSpec((CONV_K, GROUP_W), lambda i, g: (0, wxb + g)),
            pl.BlockSpec((CONV_K, D_STATE), lambda i, g: (0, wbb + g)),
            pl.BlockSpec((CONV_K, D_STATE), lambda i, g: (0, wcb + g)),
            pl.BlockSpec((1, GROUP_W), lambda i, g: (0, wxb + g)),
            pl.BlockSpec((1, D_STATE), lambda i, g: (0, wbb + g)),
            pl.BlockSpec((1, D_STATE), lambda i, g: (0, wcb + g)),
            pl.BlockSpec((1, LANES), lambda i, g: (0, 0)),
            pl.BlockSpec((1, LANES), lambda i, g: (0, 0)),
            pl.BlockSpec((1, GROUP_W), lambda i, g: (0, g)),
            pl.BlockSpec((1, GROUP_W), lambda i, g: (0, g)),
            pl.BlockSpec((rows, rows), lambda i, g: (0, 0)),
            pl.BlockSpec((1, LANES, GROUP_W), lambda i, g: (g, 0, 0)),
        ],
        out_specs=pl.BlockSpec((rows, GROUP_W), lambda i, g: (i, g)),
        scratch_shapes=[
            pltpu.VMEM((SSM_GROUPS, D_STATE, GROUP_W), F32),
            pltpu.VMEM((rows, LANES), F32),
            pltpu.VMEM((rows, LANES), F32),
            pltpu.VMEM((rows + 8, GROUP_W), F32),
        ],
        compiler_params=_cparams(("arbitrary", "arbitrary")),
        name="ssd",
    )(pf, pf, pf, pf, pf, pf, pf, dt_raw,
      conv_w, conv_w, conv_w, cb, cb, cb, dtb, aneg, dsk, nw, tri, e)


def _pad_cols(w, n):
    return jnp.pad(w, ((0, 0), (0, n - w.shape[1])))


def _even_layer(h, hb, w_in, w_gate2, b_gate, norm_w, w_out, ln_g, ln_b, valid_start, tm, tm_out):
    o = np.cumsum((0, GLA_QK, GLA_QK, GLA_V, GLA_V, GLA_GATE_RANK, SB_W, SB_W, SB_W, SB_W))
    col = lambda a: w_in[:, o[a]:o[a + 1]]
    gq, gk, gv, gr, gu, sq, sk, sv, sr = (col(a) for a in range(9))
    w_f = jnp.concatenate([gq, gk, gr, sr], axis=1).astype(BF16)
    w_b = jnp.concatenate([gv, sq, sk, sv], axis=1).astype(BF16)
    w_u = _pad_cols(gu, LANES).astype(BF16)
    pf = _matmul(hb, w_f, F32, tm, 1024)
    pb = _matmul(hb, w_b, BF16, tm, 1024)
    gu_p = _matmul(hb, w_u, F32, tm, LANES)
    wg = jnp.pad(w_gate2, ((0, LANES - GLA_GATE_RANK), (0, 0))).astype(BF16)
    o_gla = _gla(pf, pb, gu_p, wg, b_gate.reshape(1, GLA_QK), norm_w.reshape(1, GLA_V), valid_start)
    o_sb = _sb(pf, pb, valid_start)
    wo = w_out.astype(BF16)
    return _out_ln([o_gla, o_sb], [wo[:GLA_V], wo[GLA_V:]], h, ln_g, ln_b, tm_out)


def _odd_layer(h, hb, w_in, conv_w, conv_b, dt_bias, a_log, d_skip, norm_w, w_out, ln_g, ln_b,
               valid_start, tm, tm_out):
    n_main = 2 * D_INNER + 2 * SSM_GROUPS * D_STATE
    w_f = w_in[:, :n_main].astype(BF16)
    w_d = _pad_cols(w_in[:, n_main:], LANES).astype(BF16)
    pf = _matmul(hb, w_f, F32, tm, 1024)
    dt_raw = _matmul(hb, w_d, F32, tm, LANES)
    y = _ssd(pf, dt_raw, conv_w, conv_b, dt_bias, a_log, d_skip, norm_w, valid_start)
    return _out_ln([y], [w_out.astype(BF16)], h, ln_g, ln_b, tm_out)


def _trunk(h, params, valid_start, tm, tm_out):
    (ev_w_in, ev_gla_w_gate2, ev_gla_b_gate, ev_gla_norm_w, ev_w_out,
     od_w_in, od_conv_w, od_conv_b, od_dt_bias, od_a_log, od_d_skip, od_norm_w,
     od_w_out, ln_g, ln_b) = params
    hb = h.astype(BF16)
    for layer in range(DEPTH):
        j = layer // 2
        if layer % 2 == 0:
            h, hb = _even_layer(h, hb, ev_w_in[j], ev_gla_w_gate2[j], ev_gla_b_gate[j],
                                ev_gla_norm_w[j], ev_w_out[j], ln_g[layer], ln_b[layer],
                                valid_start, tm, tm_out)
        else:
            h, hb = _odd_layer(h, hb, od_w_in[j], od_conv_w[j], od_conv_b[j], od_dt_bias[j],
                               od_a_log[j], od_d_skip[j], od_norm_w[j], od_w_out[j],
                               ln_g[layer], ln_b[layer], valid_start, tm, tm_out)
    return h


def kernel(x, meta, ev_w_in, ev_gla_w_gate2, ev_gla_b_gate, ev_gla_norm_w, ev_w_out, od_w_in, od_conv_w, od_conv_b, od_dt_bias, od_a_log, od_d_skip, od_norm_w, od_w_out, ln_g, ln_b):
    assert x.shape == (1, SEQ, D_MODEL)
    front = EXTRA_PAD + PAD_FRONT
    h = jnp.concatenate([jnp.zeros((front, D_MODEL), x.dtype), meta.astype(x.dtype), x[0]], axis=0)
    params = (ev_w_in, ev_gla_w_gate2, ev_gla_b_gate, ev_gla_norm_w, ev_w_out,
              od_w_in, od_conv_w, od_conv_b, od_dt_bias, od_a_log, od_d_skip, od_norm_w,
              od_w_out, ln_g, ln_b)
    h = _trunk(h, params, valid_start=front, tm=1536, tm_out=256)
    return h[None, front + N_META:]
```

```python
import functools

import jax
import jax.numpy as jnp
import numpy as np
from jax import lax
from jax.experimental import pallas as pl
from jax.experimental.pallas import tpu as pltpu

F32 = jnp.float32
BF16 = jnp.bfloat16

D_MODEL = 2048
SEQ = 16384
DEPTH = 4
CHUNK = 64
N_META = 16
PAD_FRONT = 112

GLA_HEADS = 4
GLA_DK = 256
GLA_DV = 512
GLA_GATE_RANK = 16
GLA_TAU = 16.0
GLA_QK = GLA_HEADS * GLA_DK
GLA_V = GLA_HEADS * GLA_DV

SB_DH = 128
SB_HEADS = 8
SB_W = SB_HEADS * SB_DH

D_INNER = 4096
SSM_HEADDIM = 64
SSM_HEADS = 64
SSM_GROUPS = 8
SSM_HPG = 8
D_STATE = 128
CONV_K = 4
GROUP_W = SSM_HPG * SSM_HEADDIM

DEEPNORM_ALPHA = (2 * DEPTH) ** 0.25
LN_EPS = 1e-5
RMS_EPS = 1e-6
LOG2E = 1.4426950408889634

LANES = 128
ROW_BLOCK = 256
EXTRA_PAD = 384
VMEM_LIMIT = 56 * 1024 * 1024


def _cparams(sem):
    return pltpu.CompilerParams(dimension_semantics=sem, vmem_limit_bytes=VMEM_LIMIT)


def _dot(a, b):
    return jnp.dot(a, b, preferred_element_type=F32)


def _dot_nt(a, b):
    return lax.dot_general(a, b, (((1,), (1,)), ((), ())), preferred_element_type=F32)


def _dot_tn(a, b):
    return lax.dot_general(a, b, (((0,), (0,)), ((), ())), preferred_element_type=F32)


def _split3(x):
    hi = x.astype(BF16)
    r1 = x - hi.astype(F32)
    mid = r1.astype(BF16)
    lo = (r1 - mid.astype(F32)).astype(BF16)
    return hi, mid, lo


def _log_sigmoid(x):
    return jnp.minimum(x, 0.0) - jnp.log1p(jnp.exp(-jnp.abs(x)))


def _softplus(x):
    return jnp.maximum(x, 0.0) + jnp.log1p(jnp.exp(-jnp.abs(x)))


def _silu(x):
    return x / (1.0 + jnp.exp(-x))


def _iota(shape, dim):
    return lax.broadcasted_iota(jnp.int32, shape, dim)


def _mm_body(x_ref, w_ref, o_ref):
    o_ref[...] = _dot(x_ref[...], w_ref[...]).astype(o_ref.dtype)


def _matmul(x, w, out_dtype, tm, tn):
    m, k = x.shape
    n = w.shape[1]
    return pl.pallas_call(
        _mm_body,
        out_shape=jax.ShapeDtypeStruct((m, n), out_dtype),
        grid=(m // tm, n // tn),
        in_specs=[pl.BlockSpec((tm, k), lambda i, j: (i, 0)),
                  pl.BlockSpec((k, tn), lambda i, j: (0, j))],
        out_specs=pl.BlockSpec((tm, tn), lambda i, j: (i, j)),
        compiler_params=_cparams(("parallel", "parallel")),
        name="in_proj",
    )(x, w)


def _out_ln_body(*refs, n_lhs):
    lhs = refs[:n_lhs]
    ws = refs[n_lhs:2 * n_lhs]
    h_ref, g_ref, b_ref, o_ref, ob_ref = refs[2 * n_lhs:]
    f = _dot(lhs[0][...], ws[0][...])
    for a, w in zip(lhs[1:], ws[1:]):
        f = f + _dot(a[...], w[...])
    y = DEEPNORM_ALPHA * h_ref[...] + f
    mu = jnp.mean(y, axis=-1, keepdims=True)
    yc = y - mu
    var = jnp.mean(yc * yc, axis=-1, keepdims=True)
    out = yc * lax.rsqrt(var + LN_EPS) * g_ref[...] + b_ref[...]
    o_ref[...] = out
    ob_ref[...] = out.astype(BF16)


def _out_ln(lhs_list, w_list, h, g, b, tm):
    t, d = h.shape
    n_lhs = len(lhs_list)
    in_specs = [pl.BlockSpec((tm, a.shape[1]), lambda i: (i, 0)) for a in lhs_list]
    in_specs += [pl.BlockSpec(w.shape, lambda i: (0, 0)) for w in w_list]
    in_specs += [pl.BlockSpec((tm, d), lambda i: (i, 0)),
                 pl.BlockSpec((1, d), lambda i: (0, 0)),
                 pl.BlockSpec((1, d), lambda i: (0, 0))]
    return pl.pallas_call(
        functools.partial(_out_ln_body, n_lhs=n_lhs),
        out_shape=(jax.ShapeDtypeStruct((t, d), F32), jax.ShapeDtypeStruct((t, d), BF16)),
        grid=(t // tm,),
        in_specs=in_specs,
        out_specs=(pl.BlockSpec((tm, d), lambda i: (i, 0)), pl.BlockSpec((tm, d), lambda i: (i, 0))),
        compiler_params=_cparams(("parallel",)),
        name="out_proj_ln",
    )(*lhs_list, *w_list, h, g.reshape(1, d), b.reshape(1, d))


def _gla_body(q_ref, k_ref, v_ref, gu_ref, gr_ref, wg_ref, bg_ref, nw_ref, tri_ref,
              o_ref, st_ref, *, rows, valid_start):
    i = pl.program_id(1)

    @pl.when(i == 0)
    def _():
        st_ref[...] = jnp.zeros_like(st_ref)

    ridx = i * rows + _iota((rows, 1), 0)
    valid = ridx >= valid_start
    gate = _dot(gu_ref[...].astype(BF16), wg_ref[...]) + bg_ref[...]
    log_a = jnp.where(valid, _log_sigmoid(gate) / GLA_TAU, 0.0)
    k = jnp.where(valid, k_ref[...], 0.0)
    q = q_ref[...] * (GLA_DK ** -0.5)

    hi, mid, lo = _split3(log_a)
    p = _dot(tri_ref[...], jnp.concatenate([hi, mid, lo], axis=1))
    g_cum_all = p[:, :GLA_DK] + p[:, GLA_DK:2 * GLA_DK] + p[:, 2 * GLA_DK:]

    tmask = _iota((CHUNK, CHUNK), 0) >= _iota((CHUNK, CHUNK), 1)
    nw = nw_ref[...]
    for c in range(rows // CHUNK):
        sl = slice(c * CHUNK, (c + 1) * CHUNK)
        g_cum = g_cum_all[sl]
        g_end = g_cum[CHUNK - 1:CHUNK]
        q_dec = (q[sl] * jnp.exp(g_cum)).astype(BF16)
        k_inv = (k[sl] * jnp.exp(-g_cum)).astype(BF16)
        k_end = (k[sl] * jnp.exp(g_end - g_cum)).astype(BF16)
        vv = v_ref[sl, :]
        s = jnp.where(tmask, _dot_nt(q_dec, k_inv), 0.0)
        st = st_ref[...]
        o = _dot(s.astype(BF16), vv) + _dot_nt(q_dec, st.astype(BF16))
        st_ref[...] = st * jnp.exp(g_end) + _dot_tn(vv, k_end)
        ms = jnp.mean(o * o, axis=-1, keepdims=True)
        y = o * lax.rsqrt(ms + RMS_EPS) * nw * _silu(gr_ref[sl, :])
        o_ref[sl, :] = y.astype(BF16)


def _gla(pf, pb, gu, wg, bg, nw, valid_start):
    t = pf.shape[0]
    rows = ROW_BLOCK
    r = np.arange(rows)
    tri = ((r[:, None] >= r[None, :]) & (r[:, None] // CHUNK == r[None, :] // CHUNK))
    tri = jnp.asarray(tri, BF16)
    nkb = GLA_QK // GLA_DK
    return pl.pallas_call(
        functools.partial(_gla_body, rows=rows, valid_start=valid_start),
        out_shape=jax.ShapeDtypeStruct((t, GLA_V), BF16),
        grid=(GLA_HEADS, t // rows),
        in_specs=[
            pl.BlockSpec((rows, GLA_DK), lambda h, i: (i, h)),
            pl.BlockSpec((rows, GLA_DK), lambda h, i: (i, nkb + h)),
            pl.BlockSpec((rows, GLA_DV), lambda h, i: (i, h)),
            pl.BlockSpec((rows, LANES), lambda h, i: (i, 0)),
            pl.BlockSpec((rows, GLA_DV), lambda h, i: (i, 2 * GLA_QK // GLA_DV + h)),
            pl.BlockSpec((LANES, GLA_DK), lambda h, i: (0, h)),
            pl.BlockSpec((1, GLA_DK), lambda h, i: (0, h)),
            pl.BlockSpec((1, GLA_DV), lambda h, i: (0, h)),
            pl.BlockSpec((rows, rows), lambda h, i: (0, 0)),
        ],
        out_specs=pl.BlockSpec((rows, GLA_DV), lambda h, i: (i, h)),
        scratch_shapes=[pltpu.VMEM((GLA_DV, GLA_DK), F32)],
        compiler_params=_cparams(("parallel", "arbitrary")),
        name="gla",
    )(pf, pf, pb, gu, pf, wg, bg, nw, tri)


def _neg_abs(x):
    bits = lax.bitcast_convert_type(x, jnp.uint32) | jnp.uint32(0x80000000)
    return lax.bitcast_convert_type(bits, F32)


def _sb_body(q_ref, k_ref, v_ref, sr_ref, lneg_ref, o_ref, acc_ref, car_ref, *, bq, bk, unroll, valid_start):
    i = pl.program_id(1)
    nkq = bq // bk
    acc_ref[...] = jnp.zeros_like(acc_ref)
    car_ref[...] = jnp.zeros_like(car_ref)
    lneg = lneg_ref[...]

    def tile(j0, nblk, r0, masked):
        rows = bq - r0
        q = q_ref[r0:, :]
        start = pl.multiple_of(j0 * bk, bk)
        kb = k_ref[pl.ds(start, nblk * bk), :]
        vb = v_ref[pl.ds(start, nblk * bk), :]
        z = _dot_nt(q, kb)
        sp = jnp.maximum(z, 0.0) + jnp.log(1.0 + jnp.exp2(_neg_abs(z))) * LOG2E
        if masked:
            qi = i * bq + r0 + _iota((rows, nblk * bk), 0)
            ki = j0 * bk + _iota((rows, nblk * bk), 1)
            mask = (ki < qi) & (ki >= valid_start)
            spm = jnp.where(mask, sp, 0.0)
        else:
            spm = sp
        spb = spm.astype(BF16)
        lsig = z - sp
        car = car_ref[r0:, :]
        ws = [None] * nblk
        for b in reversed(range(nblk)):
            cs = slice(b * bk, (b + 1) * bk)
            within = _dot(spb[:, cs], lneg)
            w = jnp.exp2(lsig[:, cs] + (within + car))
            if masked:
                w = jnp.where(mask[:, cs], w, 0.0)
            ws[b] = w.astype(BF16)
            car = car + (within[:, 0:1] - spm[:, b * bk:b * bk + 1])
        w_all = ws[0] if nblk == 1 else jnp.concatenate(ws, axis=1)
        acc_ref[r0:, :] += _dot(w_all, vb)
        car_ref[r0:, :] = car

    first_kv = valid_start // bk

    for d in reversed(range(nkq)):
        @pl.when(nkq * i + d >= first_kv)
        def _(d=d):
            tile(nkq * i + d, 1, d * bk, True)

    n_plain = jnp.maximum(nkq * i - (first_kv + 1), 0)

    def body_u(n, c):
        tile(nkq * i - unroll * (n + 1), unroll, 0, False)
        return c

    lax.fori_loop(0, n_plain // unroll, body_u, 0)
    if unroll > 1:
        n_main = (n_plain // unroll) * unroll

        def body_1(n, c):
            tile(nkq * i - n_main - 1 - n, 1, 0, False)
            return c

        lax.fori_loop(0, n_plain - n_main, body_1, 0)

    @pl.when(nkq * i > first_kv)
    def _():
        tile(first_kv, 1, 0, True)

    o_ref[...] = (acc_ref[...] * _silu(sr_ref[...])).astype(BF16)


def _sb(pf, pb, valid_start, bq=512, bk=256, unroll=4):
    t = pf.shape[0]
    r = np.arange(bk)
    lneg = jnp.asarray(-(r[:, None] > r[None, :]).astype(np.float32), BF16)
    q0 = GLA_V // SB_DH
    k0 = q0 + SB_HEADS
    v0 = k0 + SB_HEADS
    r0 = (2 * GLA_QK + GLA_V) // SB_DH
    return pl.pallas_call(
        functools.partial(_sb_body, bq=bq, bk=bk, unroll=unroll, valid_start=valid_start),
        out_shape=jax.ShapeDtypeStruct((t, SB_W), BF16),
        grid=(SB_HEADS, t // bq),
        in_specs=[
            pl.BlockSpec((bq, SB_DH), lambda h, i: (i, q0 + h)),
            pl.BlockSpec((t, SB_DH), lambda h, i: (0, k0 + h)),
            pl.BlockSpec((t, SB_DH), lambda h, i: (0, v0 + h)),
            pl.BlockSpec((bq, SB_DH), lambda h, i: (i, r0 + h)),
            pl.BlockSpec((bk, bk), lambda h, i: (0, 0)),
        ],
        out_specs=pl.BlockSpec((bq, SB_DH), lambda h, i: (i, h)),
        scratch_shapes=[pltpu.VMEM((bq, SB_DH), F32), pltpu.VMEM((bq, 1), F32)],
        compiler_params=_cparams(("parallel", "arbitrary")),
        name="stick_breaking",
    )(pb, pb, pb, pf, lneg)


def _conv_silu(u, halo, w_ref, b_ref, buf_ref, rows, width):
    buf_ref[0:8, 0:width] = halo
    buf_ref[8:8 + rows, 0:width] = u
    acc = b_ref[...] + w_ref[3:4, :] * u
    for j in range(CONV_K - 1):
        acc = acc + w_ref[j:j + 1, :] * buf_ref[5 + j:5 + j + rows, 0:width]
    return _silu(acc)


def _ssd_body(z_ref, x_ref, b_ref, c_ref, xh_ref, bh_ref, ch_ref, dt_ref,
              wx_ref, wb_ref, wc_ref, bx_ref, bb_ref, bc_ref,
              dtb_ref, aneg_ref, dsk_ref, nw_ref, tri_ref, e_ref,
              o_ref, hs_ref, dts_ref, acs_ref, buf_ref, *, rows, valid_start):
    i = pl.program_id(0)
    g = pl.program_id(1)

    @pl.when(i == 0)
    def _():
        hs_ref[g] = jnp.zeros((D_STATE, GROUP_W), F32)

    @pl.when(g == 0)
    def _():
        dt = _softplus(dt_ref[...] + dtb_ref[...])
        a = dt * aneg_ref[...]
        hi, mid, lo = _split3(a)
        p = _dot(tri_ref[...], jnp.concatenate([hi, mid, lo], axis=1))
        dts_ref[...] = dt
        acs_ref[...] = p[:, :LANES] + p[:, LANES:2 * LANES] + p[:, 2 * LANES:]

    ridx = i * rows + _iota((rows, 1), 0)
    valid = ridx >= valid_start
    hvalid = (i * rows - 8 + _iota((8, 1), 0)) >= valid_start

    def prep(u_ref, h_ref, w_ref, bias_ref, width):
        u = jnp.where(valid, u_ref[...], 0.0)
        hal = jnp.where(hvalid, h_ref[...], 0.0)
        return _conv_silu(u, hal, w_ref, bias_ref, buf_ref, rows, width)

    xs = jnp.where(valid, prep(x_ref, xh_ref, wx_ref, bx_ref, GROUP_W), 0.0)
    bs = prep(b_ref, bh_ref, wb_ref, bb_ref, D_STATE)
    cs = prep(c_ref, ch_ref, wc_ref, bc_ref, D_STATE)

    lane_s = _iota((CHUNK, GROUP_W), 1) & (CHUNK - 1)
    row_t = _iota((CHUNK, GROUP_W), 0)
    diag_rep = lane_s == row_t
    causal_rep = lane_s <= row_t
    pair_mask = (_iota((LANES, LANES), 0) // CHUNK) == (_iota((LANES, LANES), 1) // CHUNK)
    e = e_ref[0]
    dsk = dsk_ref[...]
    nw = nw_ref[...]

    for c in range(rows // CHUNK):
        sl = slice(c * CHUNK, (c + 1) * CHUNK)
        both = jnp.concatenate([dts_ref[sl, :], acs_ref[sl, :]], axis=0)
        hi, mid, lo = _split3(both)
        pe = _dot(jnp.concatenate([hi, mid, lo], axis=0), e)
        ex = pe[0:2 * CHUNK] + pe[2 * CHUNK:4 * CHUNK] + pe[4 * CHUNK:6 * CHUNK]
        dt_col = ex[0:CHUNK]
        ac_col = ex[CHUNK:2 * CHUNK]
        ac_row = jnp.sum(jnp.where(diag_rep, ac_col, 0.0), axis=0, keepdims=True)
        dt_row = jnp.sum(jnp.where(diag_rep, dt_col, 0.0), axis=0, keepdims=True)
        ac_end = ac_col[CHUNK - 1:CHUNK]

        x_c = xs[sl]
        b_c = bs[sl].astype(BF16)
        c_c = cs[sl].astype(BF16)
        cb = _dot_nt(c_c, jnp.concatenate([b_c] * SSM_HPG, axis=0))
        lmat = jnp.exp(jnp.where(causal_rep, ac_col - ac_row, -jnp.inf))
        m = (lmat * cb * dt_row).astype(BF16)
        x_b = x_c.astype(BF16)
        parts = []
        for pr in range(GROUP_W // LANES):
            ps = slice(pr * LANES, (pr + 1) * LANES)
            xp = x_b[:, ps]
            xbd = jnp.where(pair_mask, jnp.concatenate([xp, xp], axis=0), 0.0).astype(BF16)
            parts.append(_dot(m[:, ps], xbd))
        y = jnp.concatenate(parts, axis=1)

        hs = hs_ref[g]
        y = y + _dot(c_c, hs.astype(BF16)) * jnp.exp(ac_col)
        xw = (x_c * dt_col * jnp.exp(ac_end - ac_col)).astype(BF16)
        hs_ref[g] = hs * jnp.exp(ac_end) + _dot_tn(b_c, xw)
        y = y + dsk * x_c

        y = y * _silu(z_ref[sl, :])
        ms = jnp.mean(y * y, axis=-1, keepdims=True)
        o_ref[sl, :] = (y * lax.rsqrt(ms + RMS_EPS) * nw).astype(BF16)


def _ssd(pf, dt_raw, conv_w, conv_b, dt_bias, a_log, d_skip, norm_w, valid_start):
    t = pf.shape[0]
    rows = ROW_BLOCK
    r = np.arange(rows)
    tri = jnp.asarray((r[:, None] >= r[None, :]) & (r[:, None] // CHUNK == r[None, :] // CHUNK), BF16)
    hh = np.arange(LANES)
    lane_head = np.arange(GROUP_W) // SSM_HEADDIM
    e = np.stack([(hh[:, None] == (gi * SSM_HPG + lane_head)[None, :]) for gi in range(SSM_GROUPS)])
    e = jnp.asarray(e, BF16)

    def pad_heads(v):
        return jnp.pad(v.astype(F32), (0, LANES - SSM_HEADS)).reshape(1, LANES)

    dtb = pad_heads(dt_bias)
    aneg = pad_heads(-jnp.exp(a_log))
    dsk = jnp.repeat(d_skip.astype(F32), SSM_HEADDIM).reshape(1, D_INNER)
    nw = norm_w.reshape(1, D_INNER)
    cb = conv_b.reshape(1, -1)
    zb = 0
    xb = D_INNER // GROUP_W
    bb = 2 * D_INNER // D_STATE
    cb0 = bb + SSM_GROUPS
    hb = rows // 8
    wxb = 0
    wbb = D_INNER // D_STATE
    wcb = wbb + SSM_GROUPS

    def halo(i):
        return jnp.maximum(i * hb - 1, 0)

    return pl.pallas_call(
        functools.partial(_ssd_body, rows=rows, valid_start=valid_start),
        out_shape=jax.ShapeDtypeStruct((t, D_INNER), BF16),
        grid=(t // rows, SSM_GROUPS),
        in_specs=[
            pl.BlockSpec((rows, GROUP_W), lambda i, g: (i, zb + g)),
            pl.BlockSpec((rows, GROUP_W), lambda i, g: (i, xb + g)),
            pl.BlockSpec((rows, D_STATE), lambda i, g: (i, bb + g)),
            pl.BlockSpec((rows, D_STATE), lambda i, g: (i, cb0 + g)),
            pl.BlockSpec((8, GROUP_W), lambda i, g: (halo(i), xb + g)),
            pl.BlockSpec((8, D_STATE), lambda i, g: (halo(i), bb + g)),
            pl.BlockSpec((8, D_STATE), lambda i, g: (halo(i), cb0 + g)),
            pl.BlockSpec((rows, LANES), lambda i, g: (i, 0)),
            pl.BlockSpec((CONV_K, GROUP_W), lambda i, g: (0, wxb + g)),
            pl.BlockSpec((CONV_K, D_STATE), lambda i, g: (0, wbb + g)),
            pl.BlockSpec((CONV_K, D_STATE), lambda i, g: (0, wcb + g)),
            pl.BlockSpec((1, GROUP_W), lambda i, g: (0, wxb + g)),
            pl.BlockSpec((1, D_STATE), lambda i, g: (0, wbb + g)),
            pl.BlockSpec((1, D_STATE), lambda i, g: (0, wcb + g)),
            pl.BlockSpec((1, LANES), lambda i, g: (0, 0)),
            pl.BlockSpec((1, LANES), lambda i, g: (0, 0)),
            pl.BlockSpec((1, GROUP_W), lambda i, g: (0, g)),
            pl.BlockSpec((1, GROUP_W), lambda i, g: (0, g)),
            pl.BlockSpec((rows, rows), lambda i, g: (0, 0)),
            pl.BlockSpec((1, LANES, GROUP_W), lambda i, g: (g, 0, 0)),
        ],
        out_specs=pl.BlockSpec((rows, GROUP_W), lambda i, g: (i, g)),
        scratch_shapes=[
            pltpu.VMEM((SSM_GROUPS, D_STATE, GROUP_W), F32),
            pltpu.VMEM((rows, LANES), F32),
            pltpu.VMEM((rows, LANES), F32),
            pltpu.VMEM((rows + 8, GROUP_W), F32),
        ],
        compiler_params=_cparams(("arbitrary", "arbitrary")),
        name="ssd",
    )(pf, pf, pf, pf, pf, pf, pf, dt_raw,
      conv_w, conv_w, conv_w, cb, cb, cb, dtb, aneg, dsk, nw, tri, e)


def _pad_cols(w, n):
    return jnp.pad(w, ((0, 0), (0, n - w.shape[1])))


def _even_layer(h, hb, w_in, w_gate2, b_gate, norm_w, w_out, ln_g, ln_b, valid_start, tm, tm_out):
    o = np.cumsum((0, GLA_QK, GLA_QK, GLA_V, GLA_V, GLA_GATE_RANK, SB_W, SB_W, SB_W, SB_W))
    col = lambda a: w_in[:, o[a]:o[a + 1]]
    gq, gk, gv, gr, gu, sq, sk, sv, sr = (col(a) for a in range(9))
    w_f = jnp.concatenate([gq, gk, gr, sr], axis=1).astype(BF16)
    sq = sq * (SB_DH ** -0.5 * LOG2E)
    w_b = jnp.concatenate([gv, sq, sk, sv], axis=1).astype(BF16)
    w_u = _pad_cols(gu, LANES).astype(BF16)
    pf = _matmul(hb, w_f, F32, tm, 1024)
    pb = _matmul(hb, w_b, BF16, tm, 1024)
    gu_p = _matmul(hb, w_u, F32, tm, LANES)
    wg = jnp.pad(w_gate2, ((0, LANES - GLA_GATE_RANK), (0, 0))).astype(BF16)
    o_gla = _gla(pf, pb, gu_p, wg, b_gate.reshape(1, GLA_QK), norm_w.reshape(1, GLA_V), valid_start)
    o_sb = _sb(pf, pb, valid_start)
    wo = w_out.astype(BF16)
    return _out_ln([o_gla, o_sb], [wo[:GLA_V], wo[GLA_V:]], h, ln_g, ln_b, tm_out)


def _odd_layer(h, hb, w_in, conv_w, conv_b, dt_bias, a_log, d_skip, norm_w, w_out, ln_g, ln_b,
               valid_start, tm, tm_out):
    n_main = 2 * D_INNER + 2 * SSM_GROUPS * D_STATE
    w_f = w_in[:, :n_main].astype(BF16)
    w_d = _pad_cols(w_in[:, n_main:], LANES).astype(BF16)
    pf = _matmul(hb, w_f, F32, tm, 1024)
    dt_raw = _matmul(hb, w_d, F32, tm, LANES)
    y = _ssd(pf, dt_raw, conv_w, conv_b, dt_bias, a_log, d_skip, norm_w, valid_start)
    return _out_ln([y], [w_out.astype(BF16)], h, ln_g, ln_b, tm_out)


def _trunk(h, params, valid_start, tm, tm_out):
    (ev_w_in, ev_gla_w_gate2, ev_gla_b_gate, ev_gla_norm_w, ev_w_out,
     od_w_in, od_conv_w, od_conv_b, od_dt_bias, od_a_log, od_d_skip, od_norm_w,
     od_w_out, ln_g, ln_b) = params
    hb = h.astype(BF16)
    for layer in range(DEPTH):
        j = layer // 2
        if layer % 2 == 0:
            h, hb = _even_layer(h, hb, ev_w_in[j], ev_gla_w_gate2[j], ev_gla_b_gate[j],
                                ev_gla_norm_w[j], ev_w_out[j], ln_g[layer], ln_b[layer],
                                valid_start, tm, tm_out)
        else:
            h, hb = _odd_layer(h, hb, od_w_in[j], od_conv_w[j], od_conv_b[j], od_dt_bias[j],
                               od_a_log[j], od_d_skip[j], od_norm_w[j], od_w_out[j],
                               ln_g[layer], ln_b[layer], valid_start, tm, tm_out)
    return h


def kernel(x, meta, ev_w_in, ev_gla_w_gate2, ev_gla_b_gate, ev_gla_norm_w, ev_w_out, od_w_in, od_conv_w, od_conv_b, od_dt_bias, od_a_log, od_d_skip, od_norm_w, od_w_out, ln_g, ln_b):
    assert x.shape == (1, SEQ, D_MODEL)
    front = EXTRA_PAD + PAD_FRONT
    h = jnp.concatenate([jnp.zeros((front, D_MODEL), x.dtype), meta.astype(x.dtype), x[0]], axis=0)
    params = (ev_w_in, ev_gla_w_gate2, ev_gla_b_gate, ev_gla_norm_w, ev_w_out,
              od_w_in, od_conv_w, od_conv_b, od_dt_bias, od_a_log, od_d_skip, od_norm_w,
              od_w_out, ln_g, ln_b)
    h = _trunk(h, params, valid_start=front, tm=1536, tm_out=256)
    return h[None, front + N_META:]
```

```python
import functools

import jax
import jax.numpy as jnp
import numpy as np
from jax import lax
from jax.experimental import pallas as pl
from jax.experimental.pallas import tpu as pltpu

F32 = jnp.float32
BF16 = jnp.bfloat16

D_MODEL = 2048
SEQ = 16384
DEPTH = 4
CHUNK = 64
N_META = 16
PAD_FRONT = 112

GLA_HEADS = 4
GLA_DK = 256
GLA_DV = 512
GLA_GATE_RANK = 16
GLA_TAU = 16.0
GLA_QK = GLA_HEADS * GLA_DK
GLA_V = GLA_HEADS * GLA_DV

SB_DH = 128
SB_HEADS = 8
SB_W = SB_HEADS * SB_DH

D_INNER = 4096
SSM_HEADDIM = 64
SSM_HEADS = 64
SSM_GROUPS = 8
SSM_HPG = 8
D_STATE = 128
CONV_K = 4
GROUP_W = SSM_HPG * SSM_HEADDIM

DEEPNORM_ALPHA = (2 * DEPTH) ** 0.25
LN_EPS = 1e-5
RMS_EPS = 1e-6
LOG2E = 1.4426950408889634

LANES = 128
ROW_BLOCK = 256
EXTRA_PAD = 384
SSD_GROUPS_PER_STEP = 2
VMEM_LIMIT = 56 * 1024 * 1024


def _cparams(sem, flags=None):
    return pltpu.CompilerParams(dimension_semantics=sem, vmem_limit_bytes=VMEM_LIMIT, flags=flags)


def _dot(a, b):
    return jnp.dot(a, b, preferred_element_type=F32)


def _dot_nt(a, b):
    return lax.dot_general(a, b, (((1,), (1,)), ((), ())), preferred_element_type=F32)


def _dot_tn(a, b):
    return lax.dot_general(a, b, (((0,), (0,)), ((), ())), preferred_element_type=F32)


def _split3(x):
    hi = x.astype(BF16)
    r1 = x - hi.astype(F32)
    mid = r1.astype(BF16)
    lo = (r1 - mid.astype(F32)).astype(BF16)
    return hi, mid, lo


def _log_sigmoid(x):
    return jnp.minimum(x, 0.0) - jnp.log1p(jnp.exp(-jnp.abs(x)))


def _softplus(x):
    return jnp.maximum(x, 0.0) + jnp.log1p(jnp.exp(-jnp.abs(x)))


def _silu(x):
    hx = 0.5 * x
    return hx + hx * jnp.tanh(hx)


def _iota(shape, dim):
    return lax.broadcasted_iota(jnp.int32, shape, dim)


def _mm_body(x_ref, w_ref, o_ref):
    o_ref[...] = _dot(x_ref[...], w_ref[...]).astype(o_ref.dtype)


def _matmul(x, w, out_dtype, tm, tn):
    m, k = x.shape
    n = w.shape[1]
    return pl.pallas_call(
        _mm_body,
        out_shape=jax.ShapeDtypeStruct((m, n), out_dtype),
        grid=(m // tm, n // tn),
        in_specs=[pl.BlockSpec((tm, k), lambda i, j: (i, 0)),
                  pl.BlockSpec((k, tn), lambda i, j: (0, j))],
        out_specs=pl.BlockSpec((tm, tn), lambda i, j: (i, j)),
        compiler_params=_cparams(("parallel", "parallel")),
        name="in_proj",
    )(x, w)


def _out_ln_body(*refs, n_lhs):
    lhs = refs[:n_lhs]
    ws = refs[n_lhs:2 * n_lhs]
    h_ref, g_ref, b_ref, o_ref, ob_ref = refs[2 * n_lhs:]
    f = _dot(lhs[0][...], ws[0][...])
    for a, w in zip(lhs[1:], ws[1:]):
        f = f + _dot(a[...], w[...])
    y = DEEPNORM_ALPHA * h_ref[...] + f
    mu = jnp.mean(y, axis=-1, keepdims=True)
    yc = y - mu
    var = jnp.mean(yc * yc, axis=-1, keepdims=True)
    out = yc * lax.rsqrt(var + LN_EPS) * g_ref[...] + b_ref[...]
    o_ref[...] = out
    ob_ref[...] = out.astype(BF16)


def _out_ln(lhs_list, w_list, h, g, b, tm):
    t, d = h.shape
    n_lhs = len(lhs_list)
    in_specs = [pl.BlockSpec((tm, a.shape[1]), lambda i: (i, 0)) for a in lhs_list]
    in_specs += [pl.BlockSpec(w.shape, lambda i: (0, 0)) for w in w_list]
    in_specs += [pl.BlockSpec((tm, d), lambda i: (i, 0)),
                 pl.BlockSpec((1, d), lambda i: (0, 0)),
                 pl.BlockSpec((1, d), lambda i: (0, 0))]
    return pl.pallas_call(
        functools.partial(_out_ln_body, n_lhs=n_lhs),
        out_shape=(jax.ShapeDtypeStruct((t, d), F32), jax.ShapeDtypeStruct((t, d), BF16)),
        grid=(t // tm,),
        in_specs=in_specs,
        out_specs=(pl.BlockSpec((tm, d), lambda i: (i, 0)), pl.BlockSpec((tm, d), lambda i: (i, 0))),
        compiler_params=_cparams(("parallel",)),
        name="out_proj_ln",
    )(*lhs_list, *w_list, h, g.reshape(1, d), b.reshape(1, d))


def _gla_body(q_ref, k_ref, v_ref, gu_ref, gr_ref, wg_ref, bg_ref, nw_ref, tri_ref,
              o_ref, st_ref, *, rows, valid_start):
    i = pl.program_id(0)

    @pl.when(i == 0)
    def _():
        st_ref[...] = jnp.zeros_like(st_ref)

    ridx = i * rows + _iota((rows, 1), 0)
    valid = ridx >= valid_start
    gate = _dot(gu_ref[...].astype(BF16), wg_ref[...]) + bg_ref[...]
    log_a = jnp.where(valid, _log_sigmoid(gate) / GLA_TAU, 0.0)
    k = jnp.where(valid, k_ref[...], 0.0)
    q = q_ref[...] * (GLA_DK ** -0.5)

    hi, mid, lo = _split3(log_a)
    p = _dot(tri_ref[...], jnp.concatenate([hi, mid, lo], axis=1))
    g_cum_all = p[:, :GLA_QK] + p[:, GLA_QK:2 * GLA_QK] + p[:, 2 * GLA_QK:]

    tmask = _iota((CHUNK, CHUNK), 0) >= _iota((CHUNK, CHUNK), 1)
    for c in range(rows // CHUNK):
        sl = slice(c * CHUNK, (c + 1) * CHUNK)
        for h in range(GLA_HEADS):
            ks = slice(h * GLA_DK, (h + 1) * GLA_DK)
            vs = slice(h * GLA_DV, (h + 1) * GLA_DV)
            g_cum = g_cum_all[sl, ks]
            g_end = g_cum[CHUNK - 1:CHUNK]
            q_dec = (q[sl, ks] * jnp.exp(g_cum)).astype(BF16)
            k_inv = (k[sl, ks] * jnp.exp(-g_cum)).astype(BF16)
            k_end = (k[sl, ks] * jnp.exp(g_end - g_cum)).astype(BF16)
            vv = v_ref[sl, vs]
            s = jnp.where(tmask, _dot_nt(q_dec, k_inv), 0.0)
            st = st_ref[h]
            o = _dot(s.astype(BF16), vv) + _dot_nt(q_dec, st.astype(BF16))
            st_ref[h] = st * jnp.exp(g_end) + _dot_tn(vv, k_end)
            ms = jnp.mean(o * o, axis=-1, keepdims=True)
            y = o * lax.rsqrt(ms + RMS_EPS) * nw_ref[:, vs] * _silu(gr_ref[sl, vs])
            o_ref[sl, vs] = y.astype(BF16)


def _gla(pf, pb, gu, wg, bg, nw, valid_start):
    t = pf.shape[0]
    rows = ROW_BLOCK
    r = np.arange(rows)
    tri = ((r[:, None] >= r[None, :]) & (r[:, None] // CHUNK == r[None, :] // CHUNK))
    tri = jnp.asarray(tri, BF16)
    return pl.pallas_call(
        functools.partial(_gla_body, rows=rows, valid_start=valid_start),
        out_shape=jax.ShapeDtypeStruct((t, GLA_V), BF16),
        grid=(t // rows,),
        in_specs=[
            pl.BlockSpec((rows, GLA_QK), lambda i: (i, 0)),
            pl.BlockSpec((rows, GLA_QK), lambda i: (i, 1)),
            pl.BlockSpec((rows, GLA_V), lambda i: (i, 0)),
            pl.BlockSpec((rows, LANES), lambda i: (i, 0)),
            pl.BlockSpec((rows, GLA_V), lambda i: (i, 2 * GLA_QK // GLA_V)),
            pl.BlockSpec((LANES, GLA_QK), lambda i: (0, 0)),
            pl.BlockSpec((1, GLA_QK), lambda i: (0, 0)),
            pl.BlockSpec((1, GLA_V), lambda i: (0, 0)),
            pl.BlockSpec((rows, rows), lambda i: (0, 0)),
        ],
        out_specs=pl.BlockSpec((rows, GLA_V), lambda i: (i, 0)),
        scratch_shapes=[pltpu.VMEM((GLA_HEADS, GLA_DV, GLA_DK), F32)],
        compiler_params=_cparams(("arbitrary",)),
        name="gla",
    )(pf, pf, pb, gu, pf, wg, bg, nw, tri)


def _neg_abs(x):
    bits = lax.bitcast_convert_type(x, jnp.uint32) | jnp.uint32(0x80000000)
    return lax.bitcast_convert_type(bits, F32)


def _sb_body(q_ref, k_ref, vt_ref, sr_ref, uneg_ref, o_ref, acc_ref, car_ref, *, bq, bk, unroll, valid_start):
    i = pl.program_id(1)
    nkq = bq // bk
    acc_ref[...] = jnp.zeros_like(acc_ref)
    car_ref[...] = jnp.zeros_like(car_ref)
    uneg = uneg_ref[...]

    def tile(j0, nblk, c0, masked):
        cols = bq - c0
        start = pl.multiple_of(j0 * bk, bk)
        z = _dot_nt(k_ref[pl.ds(start, nblk * bk), :], q_ref[c0:, :])
        sp = jnp.maximum(z, 0.0) + jnp.log(1.0 + jnp.exp2(_neg_abs(z))) * LOG2E
        if masked:
            ki = j0 * bk + _iota((nblk * bk, cols), 0)
            qi = i * bq + c0 + _iota((nblk * bk, cols), 1)
            mask = (ki < qi) & (ki >= valid_start)
            spm = jnp.where(mask, sp, 0.0)
        else:
            spm = sp
        spb = spm.astype(BF16)
        lsig = z - sp
        car = car_ref[:, c0:]
        ws = [None] * nblk
        for b in reversed(range(nblk)):
            rs = slice(b * bk, (b + 1) * bk)
            within = _dot(uneg, spb[rs, :])
            w = jnp.exp2(lsig[rs, :] + (within + car))
            if masked:
                w = jnp.where(mask[rs, :], w, 0.0)
            ws[b] = w.astype(BF16)
            car = car + (within[0:1, :] - spm[b * bk:b * bk + 1, :])
        if nblk == 1:
            pv = _dot(vt_ref[0, j0], ws[0])
        else:
            vt = jnp.concatenate([vt_ref[0, j0 + b] for b in range(nblk)], axis=1)
            pv = _dot(vt, jnp.concatenate(ws, axis=0))
        acc_ref[:, c0:] += pv
        car_ref[:, c0:] = car

    first_kv = valid_start // bk

    for d in reversed(range(nkq)):
        @pl.when(nkq * i + d >= first_kv)
        def _(d=d):
            tile(nkq * i + d, 1, d * bk, True)

    n_plain = jnp.maximum(nkq * i - (first_kv + 1), 0)

    def body_u(n, c):
        tile(nkq * i - unroll * (n + 1), unroll, 0, False)
        return c

    lax.fori_loop(0, n_plain // unroll, body_u, 0)
    if unroll > 1:
        n_main = (n_plain // unroll) * unroll

        def body_1(n, c):
            tile(nkq * i - n_main - 1 - n, 1, 0, False)
            return c

        lax.fori_loop(0, n_plain - n_main, body_1, 0)

    @pl.when(nkq * i > first_kv)
    def _():
        tile(first_kv, 1, 0, True)

    o_ref[...] = (acc_ref[...].T * _silu(sr_ref[...])).astype(BF16)


def _sb(pf, pb, valid_start, bq=512, bk=256, unroll=4):
    t = pf.shape[0]
    r = np.arange(bk)
    uneg = jnp.asarray(-(r[None, :] > r[:, None]).astype(np.float32), BF16)
    q0 = GLA_V // SB_DH
    k0 = q0 + SB_HEADS
    r0 = (2 * GLA_QK + GLA_V) // SB_DH
    sv = pb[:, GLA_V + 2 * SB_W:]
    vt = sv.reshape(t // bk, bk, SB_HEADS, SB_DH).transpose(2, 0, 3, 1)
    return pl.pallas_call(
        functools.partial(_sb_body, bq=bq, bk=bk, unroll=unroll, valid_start=valid_start),
        out_shape=jax.ShapeDtypeStruct((t, SB_W), BF16),
        grid=(SB_HEADS, t // bq),
        in_specs=[
            pl.BlockSpec((bq, SB_DH), lambda h, i: (i, q0 + h)),
            pl.BlockSpec((t, SB_DH), lambda h, i: (0, k0 + h)),
            pl.BlockSpec((1, t // bk, SB_DH, bk), lambda h, i: (h, 0, 0, 0)),
            pl.BlockSpec((bq, SB_DH), lambda h, i: (i, r0 + h)),
            pl.BlockSpec((bk, bk), lambda h, i: (0, 0)),
        ],
        out_specs=pl.BlockSpec((bq, SB_DH), lambda h, i: (i, h)),
        scratch_shapes=[pltpu.VMEM((SB_DH, bq), F32), pltpu.VMEM((1, bq), F32)],
        compiler_params=_cparams(("parallel", "arbitrary")),
        name="stick_breaking",
    )(pb, pb, vt, pf, uneg)


def _conv_silu(u, halo, w_ref, b_ref, buf_ref, out_ref, rows, width, out_valid=None):
    buf_ref[0:8, 0:width] = halo
    buf_ref[8:8 + rows, 0:width] = u
    acc = b_ref[...] + w_ref[3:4, :] * u
    for j in range(CONV_K - 1):
        acc = acc + w_ref[j:j + 1, :] * buf_ref[5 + j:5 + j + rows, 0:width]
    y = _silu(acc)
    if out_valid is not None:
        y = jnp.where(out_valid, y, 0.0)
    out_ref[...] = y


def _ssd_body(z_ref, x_ref, b_ref, c_ref, xh_ref, bh_ref, ch_ref, dt_ref,
              wx_ref, wb_ref, wc_ref, bx_ref, bb_ref, bc_ref,
              dtb_ref, aneg_ref, dsk_ref, nw_ref, tri_ref, e_ref,
              o_ref, hs_ref, dts_ref, acs_ref, buf_ref, xs_ref, bs_ref, cs_ref, *, rows, gs, valid_start):
    i = pl.program_id(0)
    g0 = pl.program_id(1) * gs

    @pl.when(i == 0)
    def _():
        for gl in range(gs):
            hs_ref[g0 + gl] = jnp.zeros((D_STATE, GROUP_W), F32)

    @pl.when(g0 == 0)
    def _():
        dt = _softplus(dt_ref[...] + dtb_ref[...])
        a = dt * aneg_ref[...]
        hi, mid, lo = _split3(a)
        p = _dot(tri_ref[...], jnp.concatenate([hi, mid, lo], axis=1))
        dts_ref[...] = dt
        acs_ref[...] = p[:, :LANES] + p[:, LANES:2 * LANES] + p[:, 2 * LANES:]

    ridx = i * rows + _iota((rows, 1), 0)
    valid = ridx >= valid_start
    hvalid = (i * rows - 8 + _iota((8, 1), 0)) >= valid_start

    def prep(u_ref, h_ref, w_ref, bias_ref, out_ref, width, out_valid=None):
        u = jnp.where(valid, u_ref[...], 0.0)
        hal = jnp.where(hvalid, h_ref[...], 0.0)
        _conv_silu(u, hal, w_ref, bias_ref, buf_ref, out_ref, rows, width, out_valid)

    prep(x_ref, xh_ref, wx_ref, bx_ref, xs_ref, gs * GROUP_W, valid)
    prep(b_ref, bh_ref, wb_ref, bb_ref, bs_ref, gs * D_STATE)
    prep(c_ref, ch_ref, wc_ref, bc_ref, cs_ref, gs * D_STATE)

    lane_s = _iota((CHUNK, GROUP_W), 1) & (CHUNK - 1)
    row_t = _iota((CHUNK, GROUP_W), 0)
    diag_rep = lane_s == row_t
    causal_rep = lane_s <= row_t
    pair_mask = (_iota((LANES, LANES), 0) // CHUNK) == (_iota((LANES, LANES), 1) // CHUNK)

    for c, gl in [(c, gl) for c in range(rows // CHUNK) for gl in range(gs)]:
        sl = slice(c * CHUNK, (c + 1) * CHUNK)
        gw = slice(gl * GROUP_W, (gl + 1) * GROUP_W)
        gn = slice(gl * D_STATE, (gl + 1) * D_STATE)
        e = e_ref[gl]
        both = jnp.concatenate([dts_ref[sl, :], acs_ref[sl, :]], axis=0)
        hi = both.astype(BF16)
        lo = (both - hi.astype(F32)).astype(BF16)
        pe = _dot(jnp.concatenate([hi, lo], axis=0), e)
        ex = pe[0:2 * CHUNK] + pe[2 * CHUNK:4 * CHUNK]
        dt_col = ex[0:CHUNK]
        ac_col = ex[CHUNK:2 * CHUNK]
        ac_row = jnp.sum(jnp.where(diag_rep, ac_col, 0.0), axis=0, keepdims=True)
        dt_row = jnp.sum(jnp.where(diag_rep, dt_col, 0.0), axis=0, keepdims=True)
        ac_end = ac_col[CHUNK - 1:CHUNK]

        x_c = xs_ref[sl, gw]
        b_c = bs_ref[sl, gn].astype(BF16)
        c_c = cs_ref[sl, gn].astype(BF16)
        cb = _dot_nt(c_c, jnp.concatenate([b_c] * SSM_HPG, axis=0))
        lmat = jnp.exp(jnp.where(causal_rep, ac_col - ac_row, -jnp.inf))
        m = (lmat * cb * dt_row).astype(BF16)
        x_b = x_c.astype(BF16)
        parts = []
        for pr in range(GROUP_W // LANES):
            ps = slice(pr * LANES, (pr + 1) * LANES)
            xp = x_b[:, ps]
            xbd = jnp.where(pair_mask, jnp.concatenate([xp, xp], axis=0), 0.0).astype(BF16)
            parts.append(_dot(m[:, ps], xbd))
        y = jnp.concatenate(parts, axis=1)

        hs = hs_ref[g0 + gl]
        y = y + _dot(c_c, hs.astype(BF16)) * jnp.exp(ac_col)
        xw = (x_c * dt_col * jnp.exp(ac_end - ac_col)).astype(BF16)
        hs_ref[g0 + gl] = hs * jnp.exp(ac_end) + _dot_tn(b_c, xw)
        y = y + dsk_ref[:, gw] * x_c

        y = y * _silu(z_ref[sl, gw])
        ms = jnp.mean(y * y, axis=-1, keepdims=True)
        o_ref[sl, gw] = (y * lax.rsqrt(ms + RMS_EPS) * nw_ref[:, gw]).astype(BF16)


def _ssd(pf, dt_raw, conv_w, conv_b, dt_bias, a_log, d_skip, norm_w, valid_start):
    t = pf.shape[0]
    rows = ROW_BLOCK
    gs = SSD_GROUPS_PER_STEP
    gw, gn = gs * GROUP_W, gs * D_STATE
    r = np.arange(rows)
    tri = jnp.asarray((r[:, None] >= r[None, :]) & (r[:, None] // CHUNK == r[None, :] // CHUNK), BF16)
    hh = np.arange(LANES)
    lane_head = np.arange(GROUP_W) // SSM_HEADDIM
    e = np.stack([(hh[:, None] == (gi * SSM_HPG + lane_head)[None, :]) for gi in range(SSM_GROUPS)])
    e = jnp.asarray(e, BF16)

    def pad_heads(v):
        return jnp.pad(v.astype(F32), (0, LANES - SSM_HEADS)).reshape(1, LANES)

    dtb = pad_heads(dt_bias)
    aneg = pad_heads(-jnp.exp(a_log))
    dsk = jnp.repeat(d_skip.astype(F32), SSM_HEADDIM).reshape(1, D_INNER)
    nw = norm_w.reshape(1, D_INNER)
    cb = conv_b.reshape(1, -1)
    zb = 0
    xb = D_INNER // gw
    bb = 2 * D_INNER // gn
    cb0 = bb + SSM_GROUPS // gs
    hb = rows // 8
    wxb = 0
    wbb = D_INNER // gn
    wcb = wbb + SSM_GROUPS // gs

    def halo(i):
        return jnp.maximum(i * hb - 1, 0)

    return pl.pallas_call(
        functools.partial(_ssd_body, rows=rows, gs=gs, valid_start=valid_start),
        out_shape=jax.ShapeDtypeStruct((t, D_INNER), BF16),
        grid=(t // rows, SSM_GROUPS // gs),
        in_specs=[
            pl.BlockSpec((rows, gw), lambda i, g: (i, zb + g)),
            pl.BlockSpec((rows, gw), lambda i, g: (i, xb + g)),
            pl.BlockSpec((rows, gn), lambda i, g: (i, bb + g)),
            pl.BlockSpec((rows, gn), lambda i, g: (i, cb0 + g)),
            pl.BlockSpec((8, gw), lambda i, g: (halo(i), xb + g)),
            pl.BlockSpec((8, gn), lambda i, g: (halo(i), bb + g)),
            pl.BlockSpec((8, gn), lambda i, g: (halo(i), cb0 + g)),
            pl.BlockSpec((rows, LANES), lambda i, g: (i, 0)),
            pl.BlockSpec((CONV_K, gw), lambda i, g: (0, wxb + g)),
            pl.BlockSpec((CONV_K, gn), lambda i, g: (0, wbb + g)),
            pl.BlockSpec((CONV_K, gn), lambda i, g: (0, wcb + g)),
            pl.BlockSpec((1, gw), lambda i, g: (0, wxb + g)),
            pl.BlockSpec((1, gn), lambda i, g: (0, wbb + g)),
            pl.BlockSpec((1, gn), lambda i, g: (0, wcb + g)),
            pl.BlockSpec((1, LANES), lambda i, g: (0, 0)),
            pl.BlockSpec((1, LANES), lambda i, g: (0, 0)),
            pl.BlockSpec((1, gw), lambda i, g: (0, g)),
            pl.BlockSpec((1, gw), lambda i, g: (0, g)),
            pl.BlockSpec((rows, rows), lambda i, g: (0, 0)),
            pl.BlockSpec((gs, LANES, GROUP_W), lambda i, g: (g, 0, 0)),
        ],
        out_specs=pl.BlockSpec((rows, gw), lambda i, g: (i, g)),
        scratch_shapes=[
            pltpu.VMEM((SSM_GROUPS, D_STATE, GROUP_W), F32),
            pltpu.VMEM((rows, LANES), F32),
            pltpu.VMEM((rows, LANES), F32),
            pltpu.VMEM((rows + 8, gw), F32),
            pltpu.VMEM((rows, gw), F32),
            pltpu.VMEM((rows, gn), F32),
            pltpu.VMEM((rows, gn), F32),
        ],
        compiler_params=_cparams(("arbitrary", "arbitrary")),
        name="ssd",
    )(pf, pf, pf, pf, pf, pf, pf, dt_raw,
      conv_w, conv_w, conv_w, cb, cb, cb, dtb, aneg, dsk, nw, tri, e)


def _pad_cols(w, n):
    return jnp.pad(w, ((0, 0), (0, n - w.shape[1])))


def _even_layer(h, hb, w_in, w_gate2, b_gate, norm_w, w_out, ln_g, ln_b, valid_start, tm, tm_out):
    o = np.cumsum((0, GLA_QK, GLA_QK, GLA_V, GLA_V, GLA_GATE_RANK, SB_W, SB_W, SB_W, SB_W))
    col = lambda a: w_in[:, o[a]:o[a + 1]]
    gq, gk, gv, gr, gu, sq, sk, sv, sr = (col(a) for a in range(9))
    w_f = jnp.concatenate([gq, gk, gr, sr], axis=1).astype(BF16)
    sq = sq * (SB_DH ** -0.5 * LOG2E)
    w_b = jnp.concatenate([gv, sq, sk, sv], axis=1).astype(BF16)
    w_u = _pad_cols(gu, LANES).astype(BF16)
    pf = _matmul(hb, w_f, F32, tm, 1024)
    pb = _matmul(hb, w_b, BF16, tm, 1024)
    gu_p = _matmul(hb, w_u, F32, tm, LANES)
    wg = jnp.pad(w_gate2, ((0, LANES - GLA_GATE_RANK), (0, 0))).astype(BF16)
    o_gla = _gla(pf, pb, gu_p, wg, b_gate.reshape(1, GLA_QK), norm_w.reshape(1, GLA_V), valid_start)
    o_sb = _sb(pf, pb, valid_start)
    wo = w_out.astype(BF16)
    return _out_ln([o_gla, o_sb], [wo[:GLA_V], wo[GLA_V:]], h, ln_g, ln_b, tm_out)


def _odd_layer(h, hb, w_in, conv_w, conv_b, dt_bias, a_log, d_skip, norm_w, w_out, ln_g, ln_b,
               valid_start, tm, tm_out):
    n_main = 2 * D_INNER + 2 * SSM_GROUPS * D_STATE
    w_f = w_in[:, :n_main].astype(BF16)
    w_d = _pad_cols(w_in[:, n_main:], LANES).astype(BF16)
    pf = _matmul(hb, w_f, F32, tm, 1024)
    dt_raw = _matmul(hb, w_d, F32, tm, LANES)
    y = _ssd(pf, dt_raw, conv_w, conv_b, dt_bias, a_log, d_skip, norm_w, valid_start)
    return _out_ln([y], [w_out.astype(BF16)], h, ln_g, ln_b, tm_out)


def _trunk(h, params, valid_start, tm, tm_out):
    (ev_w_in, ev_gla_w_gate2, ev_gla_b_gate, ev_gla_norm_w, ev_w_out,
     od_w_in, od_conv_w, od_conv_b, od_dt_bias, od_a_log, od_d_skip, od_norm_w,
     od_w_out, ln_g, ln_b) = params
    hb = h.astype(BF16)
    for layer in range(DEPTH):
        j = layer // 2
        if layer % 2 == 0:
            h, hb = _even_layer(h, hb, ev_w_in[j], ev_gla_w_gate2[j], ev_gla_b_gate[j],
                                ev_gla_norm_w[j], ev_w_out[j], ln_g[layer], ln_b[layer],
                                valid_start, tm, tm_out)
        else:
            h, hb = _odd_layer(h, hb, od_w_in[j], od_conv_w[j], od_conv_b[j], od_dt_bias[j],
                               od_a_log[j], od_d_skip[j], od_norm_w[j], od_w_out[j],
                               ln_g[layer], ln_b[layer], valid_start, tm, tm_out)
    return h


def kernel(x, meta, ev_w_in, ev_gla_w_gate2, ev_gla_b_gate, ev_gla_norm_w, ev_w_out, od_w_in, od_conv_w, od_conv_b, od_dt_bias, od_a_log, od_d_skip, od_norm_w, od_w_out, ln_g, ln_b):
    assert x.shape == (1, SEQ, D_MODEL)
    front = EXTRA_PAD + PAD_FRONT
    h = jnp.concatenate([jnp.zeros((front, D_MODEL), x.dtype), meta.astype(x.dtype), x[0]], axis=0)
    params = (ev_w_in, ev_gla_w_gate2, ev_gla_b_gate, ev_gla_norm_w, ev_w_out,
              od_w_in, od_conv_w, od_conv_b, od_dt_bias, od_a_log, od_d_skip, od_norm_w,
              od_w_out, ln_g, ln_b)
    h = _trunk(h, params, valid_start=front, tm=1536, tm_out=256)
    return h[None, front + N_META:]
```
